```python
import jax, jax.numpy as jnp
from jax import lax
import numpy as np

D_MODEL = 1024
BATCH = 8
SEQ = 2048
DEPTH = 1

MLA_HEADS = 4
MLA_Q_RANK = 384
MLA_KV_RANK = 256
MLA_NOPE = 128
MLA_ROPE = 64
MLA_V = 128
ROPE_THETA = 10000.0
ATTN_BLOCK = 128

GLA_HEADS = 4
GLA_DK = 64
GLA_DV = 128
GLA_GATE_RANK = 16
GLA_GATE_NORM = 16.0
GLA_CHUNK = 64

MIX_WIDTH = MLA_HEADS * MLA_V + GLA_HEADS * GLA_DV

IN_SIZES = (MLA_Q_RANK, MLA_KV_RANK, MLA_ROPE,
            GLA_HEADS * GLA_DK, GLA_HEADS * GLA_DK, GLA_HEADS * GLA_DV,
            GLA_GATE_RANK, GLA_GATE_RANK, GLA_HEADS * GLA_DV)
IN_WIDTH = sum(IN_SIZES)

PEER_HEADS = 8
PEER_NKEYS = 128
PEER_EXPERTS = PEER_NKEYS * PEER_NKEYS
PEER_DKEY = 256
PEER_TOPK = 16
PEER_BLOCK = 128

NORM_EPS = 1e-6

kernel_name = 'hymba_mla_gla_peer_adaln_encoder'


def rmsnorm(x, g):
    xf = x.astype(jnp.float32)
    y = xf * lax.rsqrt(jnp.mean(xf * xf, axis=-1, keepdims=True) + NORM_EPS)
    return (y * g.astype(jnp.float32)).astype(x.dtype)


def rope(x, positions):
    d = x.shape[-1]
    half = d // 2
    inv = ROPE_THETA ** (-jnp.arange(half, dtype=jnp.float32) / half)
    ang = positions.astype(jnp.float32)[..., None] * inv
    ang = ang.reshape(ang.shape[:2] + (1,) * (x.ndim - 3) + (half,))
    cos, sin = jnp.cos(ang), jnp.sin(ang)
    xf = x.astype(jnp.float32)
    x1, x2 = xf[..., :half], xf[..., half:]
    return jnp.concatenate([x1 * cos - x2 * sin, x2 * cos + x1 * sin], axis=-1).astype(x.dtype)


def block_attention(q, k, v):
    B, S, H, dq = q.shape
    nb = S // ATTN_BLOCK
    scale = dq ** -0.5
    qb = q.reshape(B, nb, ATTN_BLOCK, H, dq).transpose(1, 0, 2, 3, 4)

    def one_block(qblk):
        s = jnp.einsum('bqhd,bkhd->bhqk', qblk, k).astype(jnp.float32) * scale
        p = jax.nn.softmax(s, axis=-1).astype(v.dtype)
        return jnp.einsum('bhqk,bkhv->bqhv', p, v)

    out = lax.map(one_block, qb)
    return out.transpose(1, 0, 2, 3, 4).reshape(B, S, H, v.shape[-1])


def gla_chunked(q, k, v, log_a):
    B, S, H, dk = q.shape
    dv = v.shape[-1]
    C = GLA_CHUNK
    n = S // C

    def blocks(t):
        return t.astype(jnp.float32).reshape(B, n, C, H, t.shape[-1]).transpose(0, 3, 1, 2, 4)

    qc, kc, vc, gc = blocks(q), blocks(k), blocks(v), blocks(log_a)
    b = jnp.cumsum(gc, axis=3)
    b_last = b[:, :, :, -1:, :]
    q_dec = qc * jnp.exp(b)
    k_inv = kc * jnp.exp(-b)
    lower = jnp.tril(jnp.ones((C, C), dtype=bool))
    att = jnp.where(lower, jnp.einsum('bhnid,bhnjd->bhnij', q_dec, k_inv), 0.0)
    o_intra = jnp.einsum('bhnij,bhnjv->bhniv', att, vc)
    d_state = jnp.einsum('bhnjd,bhnjv->bhndv', kc * jnp.exp(b_last - b), vc)
    chunk_decay = jnp.exp(b_last[:, :, :, 0, :])

    def step(state, inp):
        dec, ds = inp
        return dec[..., None] * state + ds, state

    _, s_prev = lax.scan(step, jnp.zeros((B, H, dk, dv), jnp.float32),
                         (jnp.moveaxis(chunk_decay, 2, 0), jnp.moveaxis(d_state, 2, 0)))
    s_prev = jnp.moveaxis(s_prev, 0, 2)
    o = o_intra + jnp.einsum('bhnid,bhndv->bhniv', q_dec, s_prev)
    return o.transpose(0, 2, 3, 1, 4).reshape(B, S, H, dv)


def token_mixer(h, positions, w_in, g_q_latent, w_q_up, g_kv_latent, w_kv_up,
                w_gate_fwd, b_gate_fwd, w_gate_bwd, b_gate_bwd, g_gla_norm, w_out):
    B, S, _ = h.shape
    proj = h @ w_in
    offs = np.cumsum(IN_SIZES)[:-1].tolist()
    (cq, ckv, k_rope_raw, qg, kg, vg,
     gl_fwd, gl_bwd, r) = jnp.split(proj, offs, axis=-1)

    qa = (rmsnorm(cq, g_q_latent) @ w_q_up).reshape(B, S, MLA_HEADS, MLA_NOPE + MLA_ROPE)
    q_nope, q_rope = qa[..., :MLA_NOPE], rope(qa[..., MLA_NOPE:], positions)
    kv = (rmsnorm(ckv, g_kv_latent) @ w_kv_up).reshape(B, S, MLA_HEADS, MLA_NOPE + MLA_V)
    k_nope, v_mla = kv[..., :MLA_NOPE], kv[..., MLA_NOPE:]
    k_rope = rope(k_rope_raw, positions)
    q_mla = jnp.concatenate([q_nope, q_rope], axis=-1)
    k_mla = jnp.concatenate(
        [k_nope, jnp.broadcast_to(k_rope[:, :, None, :], (B, S, MLA_HEADS, MLA_ROPE))], axis=-1)
    o_mla = block_attention(q_mla, k_mla, v_mla).reshape(B, S, MLA_HEADS * MLA_V)

    qg = qg.reshape(B, S, GLA_HEADS, GLA_DK) * (GLA_DK ** -0.5)
    kg = kg.reshape(B, S, GLA_HEADS, GLA_DK)
    vg = vg.reshape(B, S, GLA_HEADS, GLA_DV)
    la_fwd = (jax.nn.log_sigmoid((gl_fwd @ w_gate_fwd + b_gate_fwd).astype(jnp.float32))
              / GLA_GATE_NORM).reshape(B, S, GLA_HEADS, GLA_DK)
    la_bwd = (jax.nn.log_sigmoid((gl_bwd @ w_gate_bwd + b_gate_bwd).astype(jnp.float32))
              / GLA_GATE_NORM).reshape(B, S, GLA_HEADS, GLA_DK)
    o_f = gla_chunked(qg, kg, vg, la_fwd)
    flip = lambda t: jnp.flip(t, axis=1)
    o_b = flip(gla_chunked(flip(qg), flip(kg), flip(vg), flip(la_bwd)))
    o_g = rmsnorm((o_f + o_b).astype(h.dtype), g_gla_norm)
    o_gla = (o_g * jax.nn.silu(r.reshape(B, S, GLA_HEADS, GLA_DV))).reshape(B, S, GLA_HEADS * GLA_DV)

    return jnp.concatenate([o_mla, o_gla], axis=-1) @ w_out


def peer(h, w_peer_q, sub_keys_1, sub_keys_2, expert_u, expert_v):
    B, S, D = h.shape
    half = PEER_DKEY // 2
    tokens = h.reshape(-1, PEER_BLOCK, D)

    def one_block(xb):
        P = xb.shape[0]
        qry = (xb @ w_peer_q).reshape(P, PEER_HEADS, PEER_DKEY)
        s1 = jnp.einsum('phd,nd->phn', qry[..., :half], sub_keys_1).astype(jnp.float32)
        s2 = jnp.einsum('phd,nd->phn', qry[..., half:], sub_keys_2).astype(jnp.float32)
        v1, i1 = lax.top_k(s1, PEER_TOPK)
        v2, i2 = lax.top_k(s2, PEER_TOPK)
        cand_s = (v1[..., :, None] + v2[..., None, :]).reshape(P, PEER_HEADS, PEER_TOPK * PEER_TOPK)
        cand_i = (i1[..., :, None] * PEER_NKEYS + i2[..., None, :]).reshape(P, PEER_HEADS, PEER_TOPK * PEER_TOPK)
        top_s, pos = lax.top_k(cand_s, PEER_TOPK)
        eidx = jnp.take_along_axis(cand_i, pos, axis=-1)
        gates = jax.nn.softmax(top_s, axis=-1)
        u = expert_u[eidx]
        a = jnp.einsum('phkd,pd->phk', u, xb).astype(jnp.float32)
        w = (gates * jax.nn.gelu(a, approximate=False)).astype(xb.dtype)
        return jnp.einsum('phk,phkd->pd', w, expert_v[eidx])

    return lax.map(one_block, tokens).reshape(B, S, D)


def setup_inputs(seed: int = 0) -> dict:
    key = jax.random.key(seed)
    ks = jax.random.split(key, 32)
    L, D = DEPTH, D_MODEL
    nrm = lambda k, shape, fan: jax.random.normal(k, shape, jnp.float32) * (fan ** -0.5)
    gain = lambda k, shape: 1.0 + 0.05 * jax.random.normal(k, shape, jnp.float32)
    x = jax.random.normal(ks[0], (BATCH, SEQ, D), jnp.float32)
    c = jax.random.normal(ks[1], (BATCH, D), jnp.float32)
    offsets = jax.random.randint(ks[2], (BATCH, 1), 0, 1024, dtype=jnp.int32)
    positions = jnp.arange(SEQ, dtype=jnp.int32)[None, :] + offsets
    return {
        'x': x,
        'c': c,
        'positions': positions,
        'w_ada': 0.5 * nrm(ks[3], (L, D, 6 * D), D),
        'b_ada': 0.01 * jax.random.normal(ks[4], (L, 6 * D), jnp.float32),
        'g_pre_mix': gain(ks[5], (L, D)),
        'g_post_mix': gain(ks[6], (L, D)),
        'w_in': nrm(ks[7], (L, D, IN_WIDTH), D),
        'g_q_latent': gain(ks[8], (L, MLA_Q_RANK)),
        'w_q_up': nrm(ks[9], (L, MLA_Q_RANK, MLA_HEADS * (MLA_NOPE + MLA_ROPE)), MLA_Q_RANK),
        'g_kv_latent': gain(ks[10], (L, MLA_KV_RANK)),
        'w_kv_up': nrm(ks[11], (L, MLA_KV_RANK, MLA_HEADS * (MLA_NOPE + MLA_V)), MLA_KV_RANK),
        'w_gate_fwd': nrm(ks[12], (L, GLA_GATE_RANK, GLA_HEADS * GLA_DK), GLA_GATE_RANK),
        'b_gate_fwd': 0.1 * jax.random.normal(ks[13], (L, GLA_HEADS * GLA_DK), jnp.float32),
        'w_gate_bwd': nrm(ks[14], (L, GLA_GATE_RANK, GLA_HEADS * GLA_DK), GLA_GATE_RANK),
        'b_gate_bwd': 0.1 * jax.random.normal(ks[15], (L, GLA_HEADS * GLA_DK), jnp.float32),
        'g_gla_norm': gain(ks[16], (L, GLA_DV)),
        'w_out': nrm(ks[17], (L, MIX_WIDTH, D), MIX_WIDTH),
        'g_pre_ffn': gain(ks[18], (L, D)),
        'g_post_ffn': gain(ks[19], (L, D)),
        'w_peer_q': nrm(ks[20], (L, D, PEER_HEADS * PEER_DKEY), D),
        'sub_keys_1': nrm(ks[21], (L, PEER_NKEYS, PEER_DKEY // 2), PEER_DKEY // 2),
        'sub_keys_2': nrm(ks[22], (L, PEER_NKEYS, PEER_DKEY // 2), PEER_DKEY // 2),
        'expert_u': nrm(ks[23], (L, PEER_EXPERTS, D), D),
        'expert_v': nrm(ks[24], (L, PEER_EXPERTS, D), D),
    }


def reference(x, c, positions, w_ada, b_ada, g_pre_mix, g_post_mix, w_in, g_q_latent, w_q_up,
              g_kv_latent, w_kv_up, w_gate_fwd, b_gate_fwd, w_gate_bwd, b_gate_bwd, g_gla_norm,
              w_out, g_pre_ffn, g_post_ffn, w_peer_q, sub_keys_1, sub_keys_2, expert_u, expert_v):
    c_act = jax.nn.silu(c)
    for l in range(DEPTH):
        mod = (c_act @ w_ada[l] + b_ada[l])[:, None, :]
        sh1, sc1, gt1, sh2, sc2, gt2 = jnp.split(mod, 6, axis=-1)
        h = rmsnorm(x, g_pre_mix[l]) * (1.0 + sc1) + sh1
        y = token_mixer(h, positions, w_in[l], g_q_latent[l], w_q_up[l], g_kv_latent[l], w_kv_up[l],
                        w_gate_fwd[l], b_gate_fwd[l], w_gate_bwd[l], b_gate_bwd[l], g_gla_norm[l], w_out[l])
        x = x + gt1 * rmsnorm(y, g_post_mix[l])
        h = rmsnorm(x, g_pre_ffn[l]) * (1.0 + sc2) + sh2
        y = peer(h, w_peer_q[l], sub_keys_1[l], sub_keys_2[l], expert_u[l], expert_v[l])
        x = x + gt2 * rmsnorm(y, g_post_ffn[l])
    return x
```

```python
import jax
import jax.numpy as jnp
import numpy as np
from jax import lax
from jax.experimental import pallas as pl
from jax.experimental.pallas import tpu as pltpu

F32 = jnp.float32
BF16 = jnp.bfloat16

D_MODEL = 1024
MLA_HEADS = 4
MLA_Q_RANK = 384
MLA_KV_RANK = 256
MLA_NOPE = 128
MLA_ROPE = 64
MLA_V = 128
ROPE_THETA = 10000.0
GLA_HEADS = 4
GLA_DK = 64
GLA_DV = 128
GLA_GATE_RANK = 16
GLA_GATE_NORM = 16.0
GLA_CHUNK = 64
PEER_HEADS = 8
PEER_NKEYS = 128
PEER_DKEY = 256
PEER_TOPK = 16
NORM_EPS = 1e-6

LANE = 128
MLA_QK = 256
VMEM_LIMIT = 56 * 1024 * 1024

TOK_TILE = 256
ATT_Q_TILE = 512
GLA_SUPER = 256
PEER_TOK = 512
PEER_EXP = 1024


def _rms(x, g):
    return x * lax.rsqrt(jnp.mean(x * x, axis=-1, keepdims=True) + NORM_EPS) * g


def _sigmoid(x):
    return 1.0 / (1.0 + jnp.exp(-x))


def _dot(a, b):
    return jnp.dot(a, b, preferred_element_type=F32)


def _dot_nt(a, b):
    return lax.dot_general(a, b, (((1,), (1,)), ((), ())), preferred_element_type=F32)


def _params(*sem):
    return pltpu.CompilerParams(dimension_semantics=sem, vmem_limit_bytes=VMEM_LIMIT)


def _ada_kernel(c_ref, w_ref, b_ref, o_ref):
    c = c_ref[...]
    c_act = (c * _sigmoid(c)).astype(BF16)
    o_ref[...] = _dot(c_act, w_ref[...].astype(BF16)) + b_ref[...]


def _ada(c, w, b):
    B, D = c.shape
    N = w.shape[1]
    tn = 768
    return pl.pallas_call(
        _ada_kernel,
        grid=(N // tn,),
        in_specs=[pl.BlockSpec((B, D), lambda j: (0, 0)),
                  pl.BlockSpec((D, tn), lambda j: (0, j)),
                  pl.BlockSpec((1, tn), lambda j: (0, j))],
        out_specs=pl.BlockSpec((B, tn), lambda j: (0, j)),
        out_shape=jax.ShapeDtypeStruct((B, N), F32),
        compiler_params=_params("arbitrary"),
        name="ada",
    )(c, w, b.reshape(1, N))


def _rope128(x, cos, sin, first):
    rot = jnp.where(first, -pltpu.roll(x, LANE - 32, axis=1), pltpu.roll(x, 32, axis=1))
    return x * cos + rot * sin


def _premix_kernel(x_ref, pos_ref, mod_ref, gpre_ref, win_ref, gq_ref, wq_ref, gkv_ref,
                   wkv_ref, wg_ref, bg_ref, inv_ref,
                   q_ref, k_ref, v_ref, qg_ref, kg_ref, vg_ref, r_ref, la_ref):
    x = x_ref[...]
    sh1 = mod_ref[0, 0:1, :]
    sc1 = mod_ref[0, 1:2, :]
    h = _rms(x, gpre_ref[...]) * (1.0 + sc1) + sh1
    proj = _dot(h.astype(BF16), win_ref[...])
    cq = proj[:, 0:384]
    ckv = proj[:, 384:640]
    qg_ref[...] = proj[:, 640:896] * (GLA_DK ** -0.5)
    kg_ref[...] = proj[:, 896:1152]
    vg_ref[...] = proj[:, 1152:1664]
    r_ref[...] = proj[:, 1664:2176]
    tail = proj[:, 2176:2304]

    ang = pos_ref[...].astype(F32) * inv_ref[...]
    cos = jnp.cos(ang)
    sin = jnp.sin(ang)
    lane = lax.broadcasted_iota(jnp.int32, ang.shape, 1)
    first = (lane & (MLA_ROPE - 1)) < (MLA_ROPE // 2)

    qa = _dot(_rms(cq, gq_ref[...]).astype(BF16), wq_ref[...])
    q_parts = []
    for hh in range(MLA_HEADS):
        q_parts.append(qa[:, hh * MLA_QK:hh * MLA_QK + MLA_NOPE])
        q_parts.append(_rope128(qa[:, hh * MLA_QK + MLA_NOPE:(hh + 1) * MLA_QK], cos, sin, first))
    q_ref[...] = jnp.concatenate(q_parts, axis=1).astype(BF16)

    kv = _dot(_rms(ckv, gkv_ref[...]).astype(BF16), wkv_ref[...])
    kr = jnp.where(lane < MLA_ROPE, _rope128(tail, cos, sin, first), 0.0)
    k_parts = []
    for hh in range(MLA_HEADS):
        k_parts.append(kv[:, hh * MLA_NOPE:(hh + 1) * MLA_NOPE])
        k_parts.append(kr)
    k_ref[...] = jnp.concatenate(k_parts, axis=1).astype(BF16)
    v_ref[...] = kv[:, 512:1024].astype(BF16)

    z = _dot(tail.astype(BF16), wg_ref[...]) + bg_ref[...]
    la = (jnp.minimum(z, 0.0) - jnp.log(1.0 + jnp.exp(-jnp.abs(z)))) * (1.0 / GLA_GATE_NORM)
    n_g = GLA_HEADS * GLA_DK
    la_ref[0] = la[:, :n_g]
    la_ref[1] = la[:, n_g:]


def _premix(x2, pos2, mod3, gpre, win_p, gq, wq_p, gkv, wkv_p, wg_p, bg_p, inv4, seq):
    T, D = x2.shape
    t = TOK_TILE
    per_b = seq // t
    row = lambda i: (i, 0)
    const = lambda i: (0, 0)
    full = lambda a: pl.BlockSpec(a.shape, const)
    outs = [(MLA_HEADS * MLA_QK, BF16), (MLA_HEADS * MLA_QK, BF16), (MLA_HEADS * MLA_V, BF16),
            (GLA_HEADS * GLA_DK, F32), (GLA_HEADS * GLA_DK, F32), (GLA_HEADS * GLA_DV, F32),
            (GLA_HEADS * GLA_DV, F32)]
    n_g = GLA_HEADS * GLA_DK
    return pl.pallas_call(
        _premix_kernel,
        grid=(T // t,),
        in_specs=[pl.BlockSpec((t, D), row),
                  pl.BlockSpec((t, 1), row),
                  pl.BlockSpec((1, 6, D), lambda i: (i // per_b, 0, 0)),
                  full(gpre), full(win_p), full(gq), full(wq_p), full(gkv), full(wkv_p),
                  full(wg_p), full(bg_p), full(inv4)],
        out_specs=[pl.BlockSpec((t, w), row) for w, _ in outs]
        + [pl.BlockSpec((2, t, n_g), lambda i: (0, i, 0))],
        out_shape=[jax.ShapeDtypeStruct((T, w), dt) for w, dt in outs]
        + [jax.ShapeDtypeStruct((2, T, n_g), F32)],
        compiler_params=_params("arbitrary"),
        name="premix",
    )(x2, pos2, mod3, gpre, win_p, gq, wq_p, gkv, wkv_p, wg_p, bg_p, inv4)


def _mla_kernel(q_ref, k_ref, v_ref, o_ref):
    scale = (MLA_NOPE + MLA_ROPE) ** -0.5
    for hh in range(MLA_HEADS):
        q = q_ref[:, hh * MLA_QK:(hh + 1) * MLA_QK]
        k = k_ref[:, hh * MLA_QK:(hh + 1) * MLA_QK]
        s = _dot_nt(q, k) * scale
        m = jnp.max(s, axis=1, keepdims=True)
        e = jnp.exp(s - m)
        l = jnp.sum(e, axis=1, keepdims=True)
        o = _dot(e.astype(BF16), v_ref[:, hh * MLA_V:(hh + 1) * MLA_V]) / l
        o_ref[:, hh * MLA_V:(hh + 1) * MLA_V] = o.astype(BF16)


def _mla(q, k, v, batch, seq):
    tq = ATT_Q_TILE
    nq = seq // tq
    return pl.pallas_call(
        _mla_kernel,
        grid=(batch, nq),
        in_specs=[pl.BlockSpec((tq, q.shape[1]), lambda b, i: (b * nq + i, 0)),
                  pl.BlockSpec((seq, k.shape[1]), lambda b, i: (b, 0)),
                  pl.BlockSpec((seq, v.shape[1]), lambda b, i: (b, 0))],
        out_specs=pl.BlockSpec((tq, v.shape[1]), lambda b, i: (b * nq + i, 0)),
        out_shape=jax.ShapeDtypeStruct((batch * seq, v.shape[1]), BF16),
        compiler_params=_params("arbitrary", "arbitrary"),
        name="mla",
    )(q, k, v)


def _gla_direction(q, k, v, g, state_ref, reverse):
    n = GLA_SUPER
    nc = n // GLA_CHUNK
    shift = GLA_CHUNK.bit_length() - 1
    row = lax.broadcasted_iota(jnp.int32, (n, n), 0)
    col = lax.broadcasted_iota(jnp.int32, (n, n), 1)
    same = (row >> shift) == (col >> shift)
    vis = same & ((col >= row) if reverse else (col <= row))
    b = jnp.dot(vis.astype(F32), g, preferred_element_type=F32, precision=lax.Precision.HIGHEST)
    tot = jnp.dot(same.astype(F32), g, preferred_element_type=F32, precision=lax.Precision.HIGHEST)
    q_dec = q * jnp.exp(b)
    k_inv = k * jnp.exp(-b)
    k_end = k * jnp.exp(tot - b)
    dec = jnp.exp(tot)

    outs = []
    for hh in range(GLA_HEADS):
        ks = slice(hh * GLA_DK, (hh + 1) * GLA_DK)
        v_h = v[:, hh * GLA_DV:(hh + 1) * GLA_DV]
        v_bf = v_h.astype(BF16)
        qd = q_dec[:, ks]
        att = jnp.where(vis, _dot_nt(qd.astype(BF16), k_inv[:, ks].astype(BF16)), 0.0)
        o = _dot(att.astype(BF16), v_bf)
        ke_bd = jnp.where(same, jnp.concatenate([k_end[:, ks]] * nc, axis=1), 0.0).astype(BF16)
        qd_bd = jnp.where(same, jnp.concatenate([qd] * nc, axis=1), 0.0).astype(BF16)
        d_all_t = _dot(jnp.transpose(v_h).astype(BF16), ke_bd)
        state = state_ref[hh]
        entering = [None] * nc
        for c in (range(nc - 1, -1, -1) if reverse else range(nc)):
            entering[c] = state
            r0 = c * GLA_CHUNK
            state = state * dec[r0:r0 + 1, ks] + d_all_t[:, r0:r0 + GLA_CHUNK]
        state_ref[hh] = state
        s_stack_t = jnp.concatenate(entering, axis=1).astype(BF16)
        outs.append(o + _dot_nt(qd_bd, s_stack_t))
    return jnp.concatenate(outs, axis=1)


def _gla_kernel(qf_ref, kf_ref, vf_ref, gf_ref, qb_ref, kb_ref, vb_ref, gb_ref,
                of_ref, ob_ref, sf_ref, sb_ref):
    @pl.when(pl.program_id(1) == 0)
    def _():
        sf_ref[...] = jnp.zeros_like(sf_ref)
        sb_ref[...] = jnp.zeros_like(sb_ref)

    of_ref[...] = _gla_direction(qf_ref[...], kf_ref[...], vf_ref[...], gf_ref[0], sf_ref, False)
    ob_ref[...] = _gla_direction(qb_ref[...], kb_ref[...], vb_ref[...], gb_ref[0], sb_ref, True)


def _gla(qg, kg, vg, la2, batch, seq):
    n = GLA_SUPER
    ns = seq // n
    T = batch * seq
    wk, wv = qg.shape[1], vg.shape[1]
    fwd = lambda b, j: (b * ns + j, 0)
    bwd = lambda b, j: (b * ns + ns - 1 - j, 0)
    state = pltpu.VMEM((GLA_HEADS, GLA_DV, GLA_DK), F32)
    return pl.pallas_call(
        _gla_kernel,
        grid=(batch, ns),
        in_specs=[pl.BlockSpec((n, wk), fwd), pl.BlockSpec((n, wk), fwd), pl.BlockSpec((n, wv), fwd),
                  pl.BlockSpec((1, n, wk), lambda b, j: (0,) + fwd(b, j)),
                  pl.BlockSpec((n, wk), bwd), pl.BlockSpec((n, wk), bwd), pl.BlockSpec((n, wv), bwd),
                  pl.BlockSpec((1, n, wk), lambda b, j: (1,) + bwd(b, j))],
        out_specs=[pl.BlockSpec((n, wv), fwd), pl.BlockSpec((n, wv), bwd)],
        out_shape=[jax.ShapeDtypeStruct((T, wv), F32), jax.ShapeDtypeStruct((T, wv), F32)],
        scratch_shapes=[state, state],
        compiler_params=_params("arbitrary", "arbitrary"),
        name="gla",
    )(qg, kg, vg, la2, qg, kg, vg, la2)


def _postmix_kernel(omla_ref, of_ref, ob_ref, r_ref, x_ref, mod_ref, ggla_ref, wout_ref, gpost_ref,
                    gffn_ref, wpq_ref, sk1_ref, sk2_ref,
                    x1_ref, h2_ref, s1_ref, s2_ref):
    og = of_ref[...] + ob_ref[...]
    r = r_ref[...]
    gate = r * _sigmoid(r)
    parts = [omla_ref[...]]
    for hh in range(GLA_HEADS):
        vs = slice(hh * GLA_DV, (hh + 1) * GLA_DV)
        parts.append((_rms(og[:, vs], ggla_ref[...]) * gate[:, vs]).astype(BF16))
    mix = jnp.concatenate(parts, axis=1)
    y = _dot(mix, wout_ref[...])
    gt1 = mod_ref[0, 2:3, :]
    sh2 = mod_ref[0, 3:4, :]
    sc2 = mod_ref[0, 4:5, :]
    x1 = x_ref[...] + gt1 * _rms(y, gpost_ref[...])
    x1_ref[...] = x1
    h2 = (_rms(x1, gffn_ref[...]) * (1.0 + sc2) + sh2).astype(BF16)
    h2_ref[...] = h2
    qry = _dot(h2, wpq_ref[...]).astype(BF16)
    half = PEER_DKEY // 2
    for hh in range(PEER_HEADS):
        c0 = hh * PEER_DKEY
        s1_ref[hh] = _dot_nt(sk1_ref[...], qry[:, c0:c0 + half])
        s2_ref[hh] = _dot_nt(sk2_ref[...], qry[:, c0 + half:c0 + PEER_DKEY])


def _postmix(omla, o_f, o_b, r, x2, mod3, ggla, wout, gpost, gffn, wpq, sk1, sk2, seq):
    T, D = x2.shape
    t = TOK_TILE
    per_b = seq // t
    row = lambda i: (i, 0)
    const = lambda i: (0, 0)
    full = lambda a: pl.BlockSpec(a.shape, const)
    return pl.pallas_call(
        _postmix_kernel,
        grid=(T // t,),
        in_specs=[pl.BlockSpec((t, omla.shape[1]), row),
                  pl.BlockSpec((t, o_f.shape[1]), row),
                  pl.BlockSpec((t, o_b.shape[1]), row),
                  pl.BlockSpec((t, r.shape[1]), row),
                  pl.BlockSpec((t, D), row),
                  pl.BlockSpec((1, 6, D), lambda i: (i // per_b, 0, 0)),
                  full(ggla), full(wout), full(gpost), full(gffn), full(wpq), full(sk1), full(sk2)],
        out_specs=[pl.BlockSpec((t, D), row),
                   pl.BlockSpec((t, D), row),
                   pl.BlockSpec((PEER_HEADS, PEER_NKEYS, t), lambda i: (0, 0, i)),
                   pl.BlockSpec((PEER_HEADS, PEER_NKEYS, t), lambda i: (0, 0, i))],
        out_shape=[jax.ShapeDtypeStruct((T, D), F32),
                   jax.ShapeDtypeStruct((T, D), BF16),
                   jax.ShapeDtypeStruct((PEER_HEADS, PEER_NKEYS, T), F32),
                   jax.ShapeDtypeStruct((PEER_HEADS, PEER_NKEYS, T), F32)],
        compiler_params=_params("arbitrary"),
        name="postmix",
    )(omla, o_f, o_b, r, x2, mod3, ggla, wout, gpost, gffn, wpq, sk1, sk2)


N_TOP = PEER_TOPK + 1
PEER_PAIRS = [(i, j) for i in range(N_TOP) for j in range(N_TOP) if (i + 1) * (j + 1) <= N_TOP]
PEER_CAND_ROWS = -(-len(PEER_PAIRS) // 8) * 8


def _top_values(s, n):
    vals = []
    for _ in range(n):
        m = jnp.max(s, axis=0, keepdims=True)
        vals.append(m)
        s = jnp.where(s == m, -jnp.inf, s)
    return vals


def _peer_route(s1, s2, cand_ref):
    v1 = _top_values(s1, N_TOP)
    v2 = _top_values(s2, N_TOP)
    cand_ref[...] = jnp.full(cand_ref.shape, -jnp.inf, F32)
    for p, (i, j) in enumerate(PEER_PAIRS):
        cand_ref[p:p + 1, :] = v1[i] + v2[j]
    cand = cand_ref[...]
    c = _top_values(cand, N_TOP)
    tau = 0.5 * (c[PEER_TOPK - 1] + c[PEER_TOPK])
    z = jnp.sum(jnp.where(cand >= tau, jnp.exp(cand - c[0]), 0.0), axis=0, keepdims=True)
    return tau - s1, jnp.exp(s1 - v1[0]), jnp.exp(s2 - v2[0]) / z


def _peer_kernel(h_ref, s1_ref, s2_ref, u_ref, vt_ref, x1_ref, mod_ref, gpost_ref,
                 out_ref, th_ref, p1_ref, p2_ref, yt_ref, at_ref, wt_ref, cand_ref):
    eb = pl.program_id(1)
    n_eb = pl.num_programs(1)
    t = h_ref.shape[0]
    rows_per_step = u_ref.shape[0] // PEER_NKEYS

    @pl.when(eb == 0)
    def _():
        yt_ref[...] = jnp.zeros_like(yt_ref)
        for hh in range(PEER_HEADS):
            th, p1, p2 = _peer_route(s1_ref[hh], s2_ref[hh], cand_ref)
            th_ref[hh] = th
            p1_ref[hh] = p1
            p2_ref[hh] = p2

    at_ref[...] = _dot_nt(u_ref[...], h_ref[...])

    i1_0 = pl.multiple_of(eb * rows_per_step, rows_per_step)

    def lane_body(c, carry):
        cs = pl.ds(pl.multiple_of(c * LANE, LANE), LANE)
        th8 = [th_ref[hh, pl.ds(i1_0, rows_per_step), cs] for hh in range(PEER_HEADS)]
        p18 = [p1_ref[hh, pl.ds(i1_0, rows_per_step), cs] for hh in range(PEER_HEADS)]
        for r in range(rows_per_step):
            g = jnp.zeros((PEER_NKEYS, LANE), F32)
            for hh in range(PEER_HEADS):
                keep = s2_ref[hh, :, cs] >= th8[hh][r:r + 1, :]
                g = g + jnp.where(keep, p2_ref[hh, :, cs] * p18[hh][r:r + 1, :], 0.0)
            a = at_ref[r * PEER_NKEYS:(r + 1) * PEER_NKEYS, cs]
            gelu = 0.5 * a * (1.0 + lax.erf(a * (2.0 ** -0.5)))
            wt_ref[r * PEER_NKEYS:(r + 1) * PEER_NKEYS, cs] = (g * gelu).astype(BF16)
        return carry

    lax.fori_loop(0, t // LANE, lane_body, 0)
    yt_ref[...] += _dot(vt_ref[...], wt_ref[...])

    @pl.when(eb == n_eb - 1)
    def _():
        y = jnp.transpose(yt_ref[...])
        gt2 = mod_ref[0, 5:6, :]
        out_ref[...] = x1_ref[...] + gt2 * _rms(y, gpost_ref[...])


def _peer(h2, s1, s2, u_bf, vt_bf, x1, mod3, gpost, seq):
    T, D = x1.shape
    E = u_bf.shape[0]
    t, te = PEER_TOK, PEER_EXP
    per_b = seq // t
    key_spec = pl.BlockSpec((PEER_HEADS, PEER_NKEYS, t), lambda i, e: (0, 0, i))
    return pl.pallas_call(
        _peer_kernel,
        grid=(T // t, E // te),
        in_specs=[pl.BlockSpec((t, D), lambda i, e: (i, 0)),
                  key_spec, key_spec,
                  pl.BlockSpec((te, D), lambda i, e: (e, 0)),
                  pl.BlockSpec((D, te), lambda i, e: (0, e)),
                  pl.BlockSpec((t, D), lambda i, e: (i, 0)),
                  pl.BlockSpec((1, 6, D), lambda i, e: (i // per_b, 0, 0)),
                  pl.BlockSpec(gpost.shape, lambda i, e: (0, 0))],
        out_specs=pl.BlockSpec((t, D), lambda i, e: (i, 0)),
        out_shape=jax.ShapeDtypeStruct((T, D), F32),
        scratch_shapes=[pltpu.VMEM((PEER_HEADS, PEER_NKEYS, t), F32),
                        pltpu.VMEM((PEER_HEADS, PEER_NKEYS, t), F32),
                        pltpu.VMEM((PEER_HEADS, PEER_NKEYS, t), F32),
                        pltpu.VMEM((D, t), F32),
                        pltpu.VMEM((te, t), F32),
                        pltpu.VMEM((te, t), BF16),
                        pltpu.VMEM((PEER_CAND_ROWS, t), F32)],
        compiler_params=_params("arbitrary", "arbitrary"),
        name="peer",
    )(h2, s1, s2, u_bf, vt_bf, x1, mod3, gpost)


def _pack_w_in(w_in):
    offs = np.cumsum([0, MLA_Q_RANK, MLA_KV_RANK, MLA_ROPE, GLA_HEADS * GLA_DK, GLA_HEADS * GLA_DK,
                      GLA_HEADS * GLA_DV, GLA_GATE_RANK, GLA_GATE_RANK, GLA_HEADS * GLA_DV])
    cq, ckv, kr, qg, kg, vg, gf, gb, r = [w_in[:, offs[i]:offs[i + 1]] for i in range(9)]
    pad = jnp.zeros((w_in.shape[0], LANE - MLA_ROPE - 2 * GLA_GATE_RANK), w_in.dtype)
    return jnp.concatenate([cq, ckv, qg, kg, vg, r, kr, gf, gb, pad], axis=1).astype(BF16)


def _pack_w_q_up(w):
    per = MLA_NOPE + MLA_ROPE
    zero = jnp.zeros((w.shape[0], MLA_QK - per), w.dtype)
    cols = []
    for hh in range(MLA_HEADS):
        cols += [w[:, hh * per:(hh + 1) * per], zero]
    return jnp.concatenate(cols, axis=1).astype(BF16)


def _pack_w_kv_up(w):
    per = MLA_NOPE + MLA_V
    ks = [w[:, hh * per:hh * per + MLA_NOPE] for hh in range(MLA_HEADS)]
    vs = [w[:, hh * per + MLA_NOPE:(hh + 1) * per] for hh in range(MLA_HEADS)]
    return jnp.concatenate(ks + vs, axis=1).astype(BF16)


def _pack_gates(wf, bf, wb, bb):
    n = GLA_HEADS * GLA_DK
    w = jnp.zeros((LANE, 2 * n), F32)
    w = w.at[MLA_ROPE:MLA_ROPE + GLA_GATE_RANK, :n].set(wf)
    w = w.at[MLA_ROPE + GLA_GATE_RANK:MLA_ROPE + 2 * GLA_GATE_RANK, n:].set(wb)
    return w.astype(BF16), jnp.concatenate([bf, bb]).reshape(1, 2 * n)


def kernel(x, c, positions, w_ada, b_ada, g_pre_mix, g_post_mix, w_in, g_q_latent, w_q_up,
           g_kv_latent, w_kv_up, w_gate_fwd, b_gate_fwd, w_gate_bwd, b_gate_bwd, g_gla_norm,
           w_out, g_pre_ffn, g_post_ffn, w_peer_q, sub_keys_1, sub_keys_2, expert_u, expert_v):
    B, S, D = x.shape
    T = B * S
    depth = w_ada.shape[0]
    half = MLA_ROPE // 2
    inv = ROPE_THETA ** (-jnp.arange(half, dtype=jnp.float32) / half)
    inv4 = jnp.tile(inv, LANE // half).reshape(1, LANE)
    pos2 = positions.reshape(T, 1)
    row = lambda a: a.reshape(1, -1)

    x2 = x.reshape(T, D)
    for l in range(depth):
        mod3 = _ada(c, w_ada[l], b_ada[l]).reshape(B, 6, D)
        wg_p, bg_p = _pack_gates(w_gate_fwd[l], b_gate_fwd[l], w_gate_bwd[l], b_gate_bwd[l])
        q, k, v, qg, kg, vg, r, la2 = _premix(
            x2, pos2, mod3, row(g_pre_mix[l]), _pack_w_in(w_in[l]), row(g_q_latent[l]),
            _pack_w_q_up(w_q_up[l]), row(g_kv_latent[l]), _pack_w_kv_up(w_kv_up[l]),
            wg_p, bg_p, inv4, S)
        o_mla = _mla(q, k, v, B, S)
        o_f, o_b = _gla(qg, kg, vg, la2, B, S)
        x1, h2, s1, s2 = _postmix(
            o_mla, o_f, o_b, r, x2, mod3, row(g_gla_norm[l]), w_out[l].astype(BF16),
            row(g_post_mix[l]), row(g_pre_ffn[l]), w_peer_q[l].astype(BF16),
            sub_keys_1[l].astype(BF16), sub_keys_2[l].astype(BF16), S)
        x2 = _peer(h2, s1, s2, expert_u[l].astype(BF16), expert_v[l].T.astype(BF16),
                   x1, mod3, row(g_post_ffn[l]), S)
    return x2.reshape(B, S, D)
```

```python
import functools

import jax
import jax.numpy as jnp
import numpy as np
from jax import lax
from jax.experimental import pallas as pl
from jax.experimental.pallas import tpu as pltpu

F32 = jnp.float32
BF16 = jnp.bfloat16

D_MODEL = 1024
MLA_HEADS = 4
MLA_Q_RANK = 384
MLA_KV_RANK = 256
MLA_NOPE = 128
MLA_ROPE = 64
MLA_V = 128
ROPE_THETA = 10000.0
GLA_HEADS = 4
GLA_DK = 64
GLA_DV = 128
GLA_GATE_RANK = 16
GLA_GATE_NORM = 16.0
GLA_CHUNK = 64
PEER_HEADS = 8
PEER_NKEYS = 128
PEER_DKEY = 256
PEER_TOPK = 16
NORM_EPS = 1e-6

LANE = 128
MLA_QK = 256
VMEM_LIMIT = 56 * 1024 * 1024

TOK_TILE = 256
ATT_Q_TILE = 512
GLA_SUPER = 256
PEER_TOK = 512
PEER_EXP = 1024
PEER_CHUNKS = 4


def _rms(x, g):
    return x * lax.rsqrt(jnp.mean(x * x, axis=-1, keepdims=True) + NORM_EPS) * g


def _sigmoid(x):
    return 1.0 / (1.0 + jnp.exp(-x))


def _dot(a, b):
    return jnp.dot(a, b, preferred_element_type=F32)


def _dot_nt(a, b):
    return lax.dot_general(a, b, (((1,), (1,)), ((), ())), preferred_element_type=F32)


def _params(*sem):
    return pltpu.CompilerParams(dimension_semantics=sem, vmem_limit_bytes=VMEM_LIMIT)


def _ada_kernel(c_ref, w_ref, b_ref, o_ref):
    c = c_ref[...]
    c_act = (c * _sigmoid(c)).astype(BF16)
    o_ref[...] = _dot(c_act, w_ref[...].astype(BF16)) + b_ref[...]


def _ada(c, w, b):
    B, D = c.shape
    N = w.shape[1]
    tn = 768
    return pl.pallas_call(
        _ada_kernel,
        grid=(N // tn,),
        in_specs=[pl.BlockSpec((B, D), lambda j: (0, 0)),
                  pl.BlockSpec((D, tn), lambda j: (0, j)),
                  pl.BlockSpec((1, tn), lambda j: (0, j))],
        out_specs=pl.BlockSpec((B, tn), lambda j: (0, j)),
        out_shape=jax.ShapeDtypeStruct((B, N), F32),
        compiler_params=_params("arbitrary"),
        name="ada",
    )(c, w, b.reshape(1, N))


def _rope128(x, cos, sin, first):
    rot = jnp.where(first, -pltpu.roll(x, LANE - 32, axis=1), pltpu.roll(x, 32, axis=1))
    return x * cos + rot * sin


def _premix_kernel(x_ref, pos_ref, mod_ref, gpre_ref, win_ref, gq_ref, wq_ref, gkv_ref,
                   wkv_ref, wg_ref, bg_ref, inv_ref,
                   q_ref, k_ref, v_ref, qg_ref, kg_ref, vg_ref, r_ref, la_ref):
    x = x_ref[...]
    sh1 = mod_ref[0, 0:1, :]
    sc1 = mod_ref[0, 1:2, :]
    h = _rms(x, gpre_ref[...]) * (1.0 + sc1) + sh1
    proj = _dot(h.astype(BF16), win_ref[...])
    cq = proj[:, 0:384]
    ckv = proj[:, 384:640]
    qg_ref[...] = proj[:, 640:896] * (GLA_DK ** -0.5)
    kg_ref[...] = proj[:, 896:1152]
    vg_ref[...] = proj[:, 1152:1664]
    r_ref[...] = proj[:, 1664:2176]
    tail = proj[:, 2176:2304]

    ang = pos_ref[...].astype(F32) * inv_ref[...]
    cos = jnp.cos(ang)
    sin = jnp.sin(ang)
    lane = lax.broadcasted_iota(jnp.int32, ang.shape, 1)
    first = (lane & (MLA_ROPE - 1)) < (MLA_ROPE // 2)

    qa = _dot(_rms(cq, gq_ref[...]).astype(BF16), wq_ref[...])
    q_parts = []
    for hh in range(MLA_HEADS):
        q_parts.append(qa[:, hh * MLA_QK:hh * MLA_QK + MLA_NOPE])
        q_parts.append(_rope128(qa[:, hh * MLA_QK + MLA_NOPE:(hh + 1) * MLA_QK], cos, sin, first))
    q_ref[...] = jnp.concatenate(q_parts, axis=1).astype(BF16)

    kv = _dot(_rms(ckv, gkv_ref[...]).astype(BF16), wkv_ref[...])
    kr = jnp.where(lane < MLA_ROPE, _rope128(tail, cos, sin, first), 0.0)
    k_parts = []
    for hh in range(MLA_HEADS):
        k_parts.append(kv[:, hh * MLA_NOPE:(hh + 1) * MLA_NOPE])
        k_parts.append(kr)
    k_ref[...] = jnp.concatenate(k_parts, axis=1).astype(BF16)
    v_ref[...] = kv[:, 512:1024].astype(BF16)

    z = _dot(tail.astype(BF16), wg_ref[...]) + bg_ref[...]
    la = (jnp.minimum(z, 0.0) - jnp.log(1.0 + jnp.exp(-jnp.abs(z)))) * (1.0 / GLA_GATE_NORM)
    n_g = GLA_HEADS * GLA_DK
    la_ref[0] = la[:, :n_g]
    la_ref[1] = la[:, n_g:]


def _premix(x2, pos2, mod3, gpre, win_p, gq, wq_p, gkv, wkv_p, wg_p, bg_p, inv4, seq):
    T, D = x2.shape
    t = TOK_TILE
    per_b = seq // t
    row = lambda i: (i, 0)
    const = lambda i: (0, 0)
    full = lambda a: pl.BlockSpec(a.shape, const)
    outs = [(MLA_HEADS * MLA_QK, BF16), (MLA_HEADS * MLA_QK, BF16), (MLA_HEADS * MLA_V, BF16),
            (GLA_HEADS * GLA_DK, F32), (GLA_HEADS * GLA_DK, F32), (GLA_HEADS * GLA_DV, F32),
            (GLA_HEADS * GLA_DV, F32)]
    n_g = GLA_HEADS * GLA_DK
    return pl.pallas_call(
        _premix_kernel,
        grid=(T // t,),
        in_specs=[pl.BlockSpec((t, D), row),
                  pl.BlockSpec((t, 1), row),
                  pl.BlockSpec((1, 6, D), lambda i: (i // per_b, 0, 0)),
                  full(gpre), full(win_p), full(gq), full(wq_p), full(gkv), full(wkv_p),
                  full(wg_p), full(bg_p), full(inv4)],
        out_specs=[pl.BlockSpec((t, w), row) for w, _ in outs]
        + [pl.BlockSpec((2, t, n_g), lambda i: (0, i, 0))],
        out_shape=[jax.ShapeDtypeStruct((T, w), dt) for w, dt in outs]
        + [jax.ShapeDtypeStruct((2, T, n_g), F32)],
        compiler_params=_params("arbitrary"),
        name="premix",
    )(x2, pos2, mod3, gpre, win_p, gq, wq_p, gkv, wkv_p, wg_p, bg_p, inv4)


def _mla_kernel(q_ref, k_ref, v_ref, o_ref):
    scale = (MLA_NOPE + MLA_ROPE) ** -0.5
    for hh in range(MLA_HEADS):
        q = q_ref[:, hh * MLA_QK:(hh + 1) * MLA_QK]
        k = k_ref[:, hh * MLA_QK:(hh + 1) * MLA_QK]
        s = _dot_nt(q, k) * scale
        m = jnp.max(s, axis=1, keepdims=True)
        e = jnp.exp(s - m)
        l = jnp.sum(e, axis=1, keepdims=True)
        o = _dot(e.astype(BF16), v_ref[:, hh * MLA_V:(hh + 1) * MLA_V]) / l
        o_ref[:, hh * MLA_V:(hh + 1) * MLA_V] = o.astype(BF16)


def _mla(q, k, v, batch, seq):
    tq = ATT_Q_TILE
    nq = seq // tq
    return pl.pallas_call(
        _mla_kernel,
        grid=(batch, nq),
        in_specs=[pl.BlockSpec((tq, q.shape[1]), lambda b, i: (b * nq + i, 0)),
                  pl.BlockSpec((seq, k.shape[1]), lambda b, i: (b, 0)),
                  pl.BlockSpec((seq, v.shape[1]), lambda b, i: (b, 0))],
        out_specs=pl.BlockSpec((tq, v.shape[1]), lambda b, i: (b * nq + i, 0)),
        out_shape=jax.ShapeDtypeStruct((batch * seq, v.shape[1]), BF16),
        compiler_params=_params("arbitrary", "arbitrary"),
        name="mla",
    )(q, k, v)


def _gla_direction(q, k, v, g, state_ref, reverse):
    n = GLA_SUPER
    nc = n // GLA_CHUNK
    shift = GLA_CHUNK.bit_length() - 1
    row = lax.broadcasted_iota(jnp.int32, (n, n), 0)
    col = lax.broadcasted_iota(jnp.int32, (n, n), 1)
    same = (row >> shift) == (col >> shift)
    vis = same & ((col >= row) if reverse else (col <= row))
    b = jnp.dot(vis.astype(F32), g, preferred_element_type=F32, precision=lax.Precision.HIGHEST)
    tot = jnp.dot(same.astype(F32), g, preferred_element_type=F32, precision=lax.Precision.HIGHEST)
    q_dec = q * jnp.exp(b)
    k_inv = k * jnp.exp(-b)
    k_end = k * jnp.exp(tot - b)
    dec = jnp.exp(tot)

    outs = []
    for hh in range(GLA_HEADS):
        ks = slice(hh * GLA_DK, (hh + 1) * GLA_DK)
        v_h = v[:, hh * GLA_DV:(hh + 1) * GLA_DV]
        v_bf = v_h.astype(BF16)
        qd = q_dec[:, ks]
        att = jnp.where(vis, _dot_nt(qd.astype(BF16), k_inv[:, ks].astype(BF16)), 0.0)
        o = _dot(att.astype(BF16), v_bf)
        ke_bd = jnp.where(same, jnp.concatenate([k_end[:, ks]] * nc, axis=1), 0.0).astype(BF16)
        qd_bd = jnp.where(same, jnp.concatenate([qd] * nc, axis=1), 0.0).astype(BF16)
        d_all_t = _dot(jnp.transpose(v_h).astype(BF16), ke_bd)
        state = state_ref[hh]
        entering = [None] * nc
        for c in (range(nc - 1, -1, -1) if reverse else range(nc)):
            entering[c] = state
            r0 = c * GLA_CHUNK
            state = state * dec[r0:r0 + 1, ks] + d_all_t[:, r0:r0 + GLA_CHUNK]
        state_ref[hh] = state
        s_stack_t = jnp.concatenate(entering, axis=1).astype(BF16)
        outs.append(o + _dot_nt(qd_bd, s_stack_t))
    return jnp.concatenate(outs, axis=1)


def _gla_kernel(qf_ref, kf_ref, vf_ref, gf_ref, qb_ref, kb_ref, vb_ref, gb_ref,
                of_ref, ob_ref, sf_ref, sb_ref):
    @pl.when(pl.program_id(1) == 0)
    def _():
        sf_ref[...] = jnp.zeros_like(sf_ref)
        sb_ref[...] = jnp.zeros_like(sb_ref)

    of_ref[...] = _gla_direction(qf_ref[...], kf_ref[...], vf_ref[...], gf_ref[0], sf_ref, False)
    ob_ref[...] = _gla_direction(qb_ref[...], kb_ref[...], vb_ref[...], gb_ref[0], sb_ref, True)


def _gla(qg, kg, vg, la2, batch, seq):
    n = GLA_SUPER
    ns = seq // n
    T = batch * seq
    wk, wv = qg.shape[1], vg.shape[1]
    fwd = lambda b, j: (b * ns + j, 0)
    bwd = lambda b, j: (b * ns + ns - 1 - j, 0)
    state = pltpu.VMEM((GLA_HEADS, GLA_DV, GLA_DK), F32)
    return pl.pallas_call(
        _gla_kernel,
        grid=(batch, ns),
        in_specs=[pl.BlockSpec((n, wk), fwd), pl.BlockSpec((n, wk), fwd), pl.BlockSpec((n, wv), fwd),
                  pl.BlockSpec((1, n, wk), lambda b, j: (0,) + fwd(b, j)),
                  pl.BlockSpec((n, wk), bwd), pl.BlockSpec((n, wk), bwd), pl.BlockSpec((n, wv), bwd),
                  pl.BlockSpec((1, n, wk), lambda b, j: (1,) + bwd(b, j))],
        out_specs=[pl.BlockSpec((n, wv), fwd), pl.BlockSpec((n, wv), bwd)],
        out_shape=[jax.ShapeDtypeStruct((T, wv), F32), jax.ShapeDtypeStruct((T, wv), F32)],
        scratch_shapes=[state, state],
        compiler_params=_params("arbitrary", "arbitrary"),
        name="gla",
    )(qg, kg, vg, la2, qg, kg, vg, la2)


def _postmix_kernel(omla_ref, of_ref, ob_ref, r_ref, x_ref, mod_ref, ggla_ref, wout_ref, gpost_ref,
                    gffn_ref, wpq_ref, sk1_ref, sk2_ref,
                    x1_ref, h2_ref, s1_ref, s2_ref):
    og = of_ref[...] + ob_ref[...]
    r = r_ref[...]
    gate = r * _sigmoid(r)
    parts = [omla_ref[...]]
    for hh in range(GLA_HEADS):
        vs = slice(hh * GLA_DV, (hh + 1) * GLA_DV)
        parts.append((_rms(og[:, vs], ggla_ref[...]) * gate[:, vs]).astype(BF16))
    mix = jnp.concatenate(parts, axis=1)
    y = _dot(mix, wout_ref[...])
    gt1 = mod_ref[0, 2:3, :]
    sh2 = mod_ref[0, 3:4, :]
    sc2 = mod_ref[0, 4:5, :]
    x1 = x_ref[...] + gt1 * _rms(y, gpost_ref[...])
    x1_ref[...] = x1
    h2 = (_rms(x1, gffn_ref[...]) * (1.0 + sc2) + sh2).astype(BF16)
    h2_ref[...] = h2
    qry = _dot(h2, wpq_ref[...]).astype(BF16)
    half = PEER_DKEY // 2
    for hh in range(PEER_HEADS):
        c0 = hh * PEER_DKEY
        s1 = _dot_nt(sk1_ref[...], qry[:, c0:c0 + half])
        s2 = _dot_nt(sk2_ref[...], qry[:, c0 + half:c0 + PEER_DKEY])
        for c in range(s1_ref.shape[0]):
            s1_ref[c, hh] = s1[:, c * LANE:(c + 1) * LANE]
            s2_ref[c, hh] = s2[:, c * LANE:(c + 1) * LANE]


def _postmix(omla, o_f, o_b, r, x2, mod3, ggla, wout, gpost, gffn, wpq, sk1, sk2, seq):
    T, D = x2.shape
    t = TOK_TILE
    per_b = seq // t
    row = lambda i: (i, 0)
    const = lambda i: (0, 0)
    full = lambda a: pl.BlockSpec(a.shape, const)
    key_spec = pl.BlockSpec((t // LANE, PEER_HEADS, PEER_NKEYS, LANE), lambda i: (i, 0, 0, 0))
    key_shape = jax.ShapeDtypeStruct((T // LANE, PEER_HEADS, PEER_NKEYS, LANE), F32)
    return pl.pallas_call(
        _postmix_kernel,
        grid=(T // t,),
        in_specs=[pl.BlockSpec((t, omla.shape[1]), row),
                  pl.BlockSpec((t, o_f.shape[1]), row),
                  pl.BlockSpec((t, o_b.shape[1]), row),
                  pl.BlockSpec((t, r.shape[1]), row),
                  pl.BlockSpec((t, D), row),
                  pl.BlockSpec((1, 6, D), lambda i: (i // per_b, 0, 0)),
                  full(ggla), full(wout), full(gpost), full(gffn), full(wpq), full(sk1), full(sk2)],
        out_specs=[pl.BlockSpec((t, D), row),
                   pl.BlockSpec((t, D), row),
                   key_spec, key_spec],
        out_shape=[jax.ShapeDtypeStruct((T, D), F32),
                   jax.ShapeDtypeStruct((T, D), BF16),
                   key_shape, key_shape],
        compiler_params=_params("arbitrary"),
        name="postmix",
    )(omla, o_f, o_b, r, x2, mod3, ggla, wout, gpost, gffn, wpq, sk1, sk2)


N_TOP = PEER_TOPK + 1
PEER_PAIRS = [(i, j) for i in range(N_TOP) for j in range(N_TOP) if (i + 1) * (j + 1) <= N_TOP]
PEER_CAND_ROWS = -(-len(PEER_PAIRS) // 8) * 8


def _top_values(s, n):
    vals = []
    for _ in range(n):
        m = jnp.max(s, axis=0, keepdims=True)
        vals.append(m)
        s = jnp.where(s == m, -jnp.inf, s)
    return vals


def _peer_route(s1, s2, cand_ref):
    v1 = _top_values(s1, N_TOP)
    v2 = _top_values(s2, N_TOP)
    cand_ref[...] = jnp.full(cand_ref.shape, -jnp.inf, F32)
    for p, (i, j) in enumerate(PEER_PAIRS):
        cand_ref[p:p + 1, :] = v1[i] + v2[j]
    cand = cand_ref[...]
    c = _top_values(cand, N_TOP)
    tau = 0.5 * (c[PEER_TOPK - 1] + c[PEER_TOPK])
    z = jnp.sum(jnp.where(cand >= tau, jnp.exp(cand - c[0]), 0.0), axis=0, keepdims=True)
    return tau - s1, jnp.exp(s1 - v1[0]), jnp.exp(s2 - v2[0]) / z


def _peer_weights(a, r0, wt_ref, th_ref, p1_ref, p2_ref, s2_ref):
    for q in range(a.shape[0] // PEER_NKEYS):
        r = r0 + q
        for c in range(a.shape[1] // LANE):
            g = jnp.zeros((PEER_NKEYS, LANE), F32)
            for hh in range(PEER_HEADS):
                th = th_ref[c, hh, r:r + 1, :]
                p1 = p1_ref[c, hh, r:r + 1, :]
                g = g + jnp.where(s2_ref[c, hh] >= th, p2_ref[c, hh] * p1, 0.0)
            a_rc = a[q * PEER_NKEYS:(q + 1) * PEER_NKEYS, c * LANE:(c + 1) * LANE]
            gelu = 0.5 * a_rc * (1.0 + lax.erf(a_rc * (2.0 ** -0.5)))
            wt_ref[r * PEER_NKEYS:(r + 1) * PEER_NKEYS, c * LANE:(c + 1) * LANE] = (g * gelu).astype(BF16)


def _peer_kernel(n_eb, n_tb, h_ref, s1_ref, s2_ref, u_ref, vta_ref, vtb_ref, x1_ref, mod_ref,
                 gpost_ref, out_ref, th_ref, p1_ref, p2_ref, th8_ref, p18_ref, yt_ref, wt0_ref,
                 wt1_ref, cand_ref):
    k = pl.program_id(0)
    te = PEER_EXP
    rows = te // PEER_NKEYS
    pairs = n_eb // 2
    n_c = s2_ref.shape[0]
    opens_block = lax.rem(k, pairs) == 0

    @pl.when(k == 0)
    def _():
        wt1_ref[...] = jnp.zeros_like(wt1_ref)
        yt_ref[...] = jnp.zeros_like(yt_ref)

    @pl.when(opens_block & (k < n_tb * pairs))
    def _():
        half_heads = PEER_HEADS // 2

        def route(i, carry):
            c = i // half_heads
            for j in range(2):
                hh = lax.rem(i, half_heads) + j * half_heads
                th, p1, p2 = _peer_route(s1_ref[c, hh], s2_ref[c, hh], cand_ref.at[j])
                th_ref[c, hh] = th
                p1_ref[c, hh] = p1
                p2_ref[c, hh] = p2
            return carry

        lax.fori_loop(0, n_c * half_heads, route, 0)

    def tile_step(u_row0, wt_new_ref, tile, vt_ref, wt_old_ref):
        i1_0 = pl.multiple_of(lax.rem(tile, n_eb) * rows, rows)
        th8_ref[...] = th_ref[:, :, pl.ds(i1_0, rows), :]
        p18_ref[...] = p1_ref[:, :, pl.ds(i1_0, rows), :]
        ce = te // PEER_CHUNKS
        cd = yt_ref.shape[0] // PEER_CHUNKS
        pre = lambda q: _dot_nt(u_ref[u_row0 + q * ce:u_row0 + (q + 1) * ce], h_ref[...])
        a_next = pre(0)
        for q in range(PEER_CHUNKS):
            a = a_next
            if q + 1 < PEER_CHUNKS:
                a_next = pre(q + 1)
            _peer_weights(a, q * (ce // PEER_NKEYS), wt_new_ref, th8_ref, p18_ref, p2_ref, s2_ref)
            yt_ref[q * cd:(q + 1) * cd] += _dot(vt_ref[q * cd:(q + 1) * cd], wt_old_ref[...])

    tile_step(0, wt0_ref, 2 * k, vta_ref, wt1_ref)

    @pl.when((k >= 1) & opens_block)
    def _():
        y = jnp.transpose(yt_ref[...])
        gt2 = mod_ref[0, 5:6, :]
        out_ref[...] = x1_ref[...] + gt2 * _rms(y, gpost_ref[...])
        yt_ref[...] = jnp.zeros_like(yt_ref)

    tile_step(te, wt1_ref, 2 * k + 1, vtb_ref, wt0_ref)


def _peer(h2, s1, s2, u_bf, vt_bf, x1, mod3, gpost, seq):
    T, D = x1.shape
    E = u_bf.shape[0]
    t, te = PEER_TOK, PEER_EXP
    n_c = t // LANE
    n_tb, n_eb = T // t, E // te
    pairs = n_eb // 2
    n_steps = n_tb * pairs + 1
    per_b = seq // t
    tb_in = lambda k: jnp.minimum(k // pairs, n_tb - 1)
    tb_out = lambda k: jnp.maximum(k - 1, 0) // pairs
    key_spec = pl.BlockSpec((n_c, PEER_HEADS, PEER_NKEYS, LANE), lambda k: (tb_in(k), 0, 0, 0))
    table = pltpu.VMEM((n_c, PEER_HEADS, PEER_NKEYS, LANE), F32)
    return pl.pallas_call(
        functools.partial(_peer_kernel, n_eb, n_tb),
        grid=(n_steps,),
        in_specs=[pl.BlockSpec((t, D), lambda k: (tb_in(k), 0)),
                  key_spec, key_spec,
                  pl.BlockSpec((2 * te, D), lambda k: (lax.rem(jnp.minimum(k, n_steps - 2), pairs), 0)),
                  pl.BlockSpec((D, te), lambda k: (0, lax.rem(jnp.maximum(2 * k - 1, 0), n_eb))),
                  pl.BlockSpec((D, te), lambda k: (0, lax.rem(2 * k, n_eb))),
                  pl.BlockSpec((t, D), lambda k: (tb_out(k), 0)),
                  pl.BlockSpec((1, 6, D), lambda k: (tb_out(k) // per_b, 0, 0)),
                  pl.BlockSpec(gpost.shape, lambda k: (0, 0))],
        out_specs=pl.BlockSpec((t, D), lambda k: (tb_out(k), 0)),
        out_shape=jax.ShapeDtypeStruct((T, D), F32),
        scratch_shapes=[table, table, table,
                        pltpu.VMEM((n_c, PEER_HEADS, te // PEER_NKEYS, LANE), F32),
                        pltpu.VMEM((n_c, PEER_HEADS, te // PEER_NKEYS, LANE), F32),
                        pltpu.VMEM((D, t), F32),
                        pltpu.VMEM((te, t), BF16),
                        pltpu.VMEM((te, t), BF16),
                        pltpu.VMEM((2, PEER_CAND_ROWS, LANE), F32)],
        compiler_params=_params("arbitrary"),
        name="peer",
    )(h2, s1, s2, u_bf, vt_bf, vt_bf, x1, mod3, gpost)


def _pack_w_in(w_in):
    offs = np.cumsum([0, MLA_Q_RANK, MLA_KV_RANK, MLA_ROPE, GLA_HEADS * GLA_DK, GLA_HEADS * GLA_DK,
                      GLA_HEADS * GLA_DV, GLA_GATE_RANK, GLA_GATE_RANK, GLA_HEADS * GLA_DV])
    cq, ckv, kr, qg, kg, vg, gf, gb, r = [w_in[:, offs[i]:offs[i + 1]] for i in range(9)]
    pad = jnp.zeros((w_in.shape[0], LANE - MLA_ROPE - 2 * GLA_GATE_RANK), w_in.dtype)
    return jnp.concatenate([cq, ckv, qg, kg, vg, r, kr, gf, gb, pad], axis=1).astype(BF16)


def _pack_w_q_up(w):
    per = MLA_NOPE + MLA_ROPE
    zero = jnp.zeros((w.shape[0], MLA_QK - per), w.dtype)
    cols = []
    for hh in range(MLA_HEADS):
        cols += [w[:, hh * per:(hh + 1) * per], zero]
    return jnp.concatenate(cols, axis=1).astype(BF16)


def _pack_w_kv_up(w):
    per = MLA_NOPE + MLA_V
    ks = [w[:, hh * per:hh * per + MLA_NOPE] for hh in range(MLA_HEADS)]
    vs = [w[:, hh * per + MLA_NOPE:(hh + 1) * per] for hh in range(MLA_HEADS)]
    return jnp.concatenate(ks + vs, axis=1).astype(BF16)


def _pack_gates(wf, bf, wb, bb):
    n = GLA_HEADS * GLA_DK
    w = jnp.zeros((LANE, 2 * n), F32)
    w = w.at[MLA_ROPE:MLA_ROPE + GLA_GATE_RANK, :n].set(wf)
    w = w.at[MLA_ROPE + GLA_GATE_RANK:MLA_ROPE + 2 * GLA_GATE_RANK, n:].set(wb)
    return w.astype(BF16), jnp.concatenate([bf, bb]).reshape(1, 2 * n)


def kernel(x, c, positions, w_ada, b_ada, g_pre_mix, g_post_mix, w_in, g_q_latent, w_q_up,
           g_kv_latent, w_kv_up, w_gate_fwd, b_gate_fwd, w_gate_bwd, b_gate_bwd, g_gla_norm,
           w_out, g_pre_ffn, g_post_ffn, w_peer_q, sub_keys_1, sub_keys_2, expert_u, expert_v):
    B, S, D = x.shape
    T = B * S
    depth = w_ada.shape[0]
    half = MLA_ROPE // 2
    inv = ROPE_THETA ** (-jnp.arange(half, dtype=jnp.float32) / half)
    inv4 = jnp.tile(inv, LANE // half).reshape(1, LANE)
    pos2 = positions.reshape(T, 1)
    row = lambda a: a.reshape(1, -1)

    x2 = x.reshape(T, D)
    for l in range(depth):
        mod3 = _ada(c, w_ada[l], b_ada[l]).reshape(B, 6, D)
        wg_p, bg_p = _pack_gates(w_gate_fwd[l], b_gate_fwd[l], w_gate_bwd[l], b_gate_bwd[l])
        q, k, v, qg, kg, vg, r, la2 = _premix(
            x2, pos2, mod3, row(g_pre_mix[l]), _pack_w_in(w_in[l]), row(g_q_latent[l]),
            _pack_w_q_up(w_q_up[l]), row(g_kv_latent[l]), _pack_w_kv_up(w_kv_up[l]),
            wg_p, bg_p, inv4, S)
        o_mla = _mla(q, k, v, B, S)
        o_f, o_b = _gla(qg, kg, vg, la2, B, S)
        x1, h2, s1, s2 = _postmix(
            o_mla, o_f, o_b, r, x2, mod3, row(g_gla_norm[l]), w_out[l].astype(BF16),
            row(g_post_mix[l]), row(g_pre_ffn[l]), w_peer_q[l].astype(BF16),
            sub_keys_1[l].astype(BF16), sub_keys_2[l].astype(BF16), S)
        x2 = _peer(h2, s1, s2, expert_u[l].astype(BF16), expert_v[l].T.astype(BF16),
                   x1, mod3, row(g_post_ffn[l]), S)
    return x2.reshape(B, S, D)
```
